```python
import math
import jax, jax.numpy as jnp
from jax import lax
import numpy as np

D_MODEL = 1024
BATCH = 16
SEQ = 2048
DEPTH = 1

CHUNK = 64
N_Q_HEADS = 8
N_KV_HEADS = 2
HEAD_DIM = 64
Q_REP = N_Q_HEADS // N_KV_HEADS
ATTN_WIDTH = N_Q_HEADS * HEAD_DIM
KV_WIDTH = N_KV_HEADS * HEAD_DIM
WINDOW = 128
WINDOW_CHUNKS = WINDOW // CHUNK
BAND = (WINDOW_CHUNKS + 1) * CHUNK
ROPE_THETA = 500000.0
ROT_DIM = HEAD_DIM // 4
SSM_WIDTH = D_MODEL - ATTN_WIDTH
SSM_GROUP = 16
N_SSM_GROUPS = SSM_WIDTH // SSM_GROUP
SSM_STATE = 64
MIX_WIDTH = ATTN_WIDTH + SSM_WIDTH
IN_WIDTH = ATTN_WIDTH + 2 * KV_WIDTH + SSM_WIDTH
D_FF = -(-8 * D_MODEL // (3 * 256)) * 256
PLE_DIM = 256
EPS = 1e-6
DT_MIN = 1e-3
DT_MAX = 1e-1
LAMBDA_RE_MAX = -1e-4
NEG_INF = -1e30

kernel_name = "hymba_swa_sink_s5_swiglu_ple"

F32 = jnp.float32


def rms_norm(t, g):
    tf = t.astype(F32)
    y = tf * lax.rsqrt(jnp.mean(tf * tf, axis=-1, keepdims=True) + EPS)
    return (y * g.astype(F32)).astype(t.dtype)


def partial_rotary(t, positions):
    half = ROT_DIM // 2
    inv_freq = ROPE_THETA ** (-jnp.arange(half, dtype=F32) * (2.0 / ROT_DIM))
    ang = positions.astype(F32)[..., None] * inv_freq
    cos = jnp.cos(ang)[:, :, None, :]
    sin = jnp.sin(ang)[:, :, None, :]
    tr = t[..., :ROT_DIM].astype(F32)
    t1, t2 = tr[..., :half], tr[..., half:]
    rot = jnp.concatenate([t1 * cos - t2 * sin, t1 * sin + t2 * cos], axis=-1).astype(t.dtype)
    return jnp.concatenate([rot, t[..., ROT_DIM:]], axis=-1)


def sliding_window_attention(q, k, v, sinks):
    B, L = q.shape[0], q.shape[1]
    nc = L // CHUNK
    qc = q.reshape(B, nc, CHUNK, N_KV_HEADS, Q_REP, HEAD_DIM)

    def band(t):
        tc = t.reshape(B, nc, CHUNK, N_KV_HEADS, HEAD_DIM)
        tp = jnp.pad(tc, ((0, 0), (WINDOW_CHUNKS, 0), (0, 0), (0, 0), (0, 0)))
        return jnp.concatenate([tp[:, i:i + nc] for i in range(WINDOW_CHUNKS + 1)], axis=2)

    kb = band(k)
    vb = band(v)
    scores = jnp.einsum('bcqhgd,bcjhd->bchgqj', qc, kb,
                        preferred_element_type=F32) * (HEAD_DIM ** -0.5)
    key_chunk = (jnp.arange(nc)[:, None] - WINDOW_CHUNKS
                 + jnp.arange(BAND)[None, :] // CHUNK)
    valid = (key_chunk >= 0)[None, :, None, None, None, :]
    scores = jnp.where(valid, scores, NEG_INF)
    sink = sinks.astype(F32).reshape(N_KV_HEADS, Q_REP)[None, None, :, :, None, None]
    m = jnp.maximum(jnp.max(scores, axis=-1, keepdims=True), sink)
    pr = jnp.exp(scores - m)
    denom = jnp.sum(pr, axis=-1, keepdims=True) + jnp.exp(sink - m)
    w = (pr / denom).astype(v.dtype)
    out = jnp.einsum('bchgqj,bcjhd->bcqhgd', w, vb)
    return out.reshape(B, L, ATTN_WIDTH)


def s5_ssm(u, lam_re, lam_im, b_re, b_im, c_re, c_im, d_skip, log_dt, glu_w, glu_b):
    Bsz, L = u.shape[0], u.shape[1]
    uf = u.astype(F32).reshape(Bsz, L, N_SSM_GROUPS, SSM_GROUP)
    lr = jnp.minimum(lam_re.astype(F32), LAMBDA_RE_MAX)
    li = lam_im.astype(F32)
    dt = jnp.exp(log_dt.astype(F32))[:, None]
    mag = jnp.exp(lr * dt)
    ab_re = mag * jnp.cos(li * dt)
    ab_im = mag * jnp.sin(li * dt)
    nr = ab_re - 1.0
    ni = ab_im
    den = lr * lr + li * li
    f_re = (nr * lr + ni * li) / den
    f_im = (ni * lr - nr * li) / den
    br = b_re.astype(F32)
    bi = b_im.astype(F32)
    bb_re = f_re[..., None] * br - f_im[..., None] * bi
    bb_im = f_re[..., None] * bi + f_im[..., None] * br
    bu_re = jnp.einsum('blgc,gnc->blgn', uf, bb_re)
    bu_im = jnp.einsum('blgc,gnc->blgn', uf, bb_im)
    a_re = jnp.broadcast_to(ab_re, (1, L, N_SSM_GROUPS, SSM_STATE))
    a_im = jnp.broadcast_to(ab_im, (1, L, N_SSM_GROUPS, SSM_STATE))

    def combine(e1, e2):
        a1r, a1i, b1r, b1i = e1
        a2r, a2i, b2r, b2i = e2
        return (a2r * a1r - a2i * a1i,
                a2r * a1i + a2i * a1r,
                a2r * b1r - a2i * b1i + b2r,
                a2r * b1i + a2i * b1r + b2i)

    _, _, s_re, s_im = lax.associative_scan(combine, (a_re, a_im, bu_re, bu_im), axis=1)
    y = (jnp.einsum('blgn,gcn->blgc', s_re, c_re.astype(F32))
         - jnp.einsum('blgn,gcn->blgc', s_im, c_im.astype(F32))
         + d_skip.astype(F32) * uf)
    y = jax.nn.gelu(y.reshape(Bsz, L, SSM_WIDTH)).astype(u.dtype)
    return y * jax.nn.sigmoid(y @ glu_w + glu_b)


def setup_inputs(seed: int = 0) -> dict:
    key = jax.random.key(seed)
    ks = jax.random.split(key, 32)
    nrm = jax.random.normal
    x = nrm(ks[0], (BATCH, SEQ, D_MODEL), F32)
    p = nrm(ks[1], (DEPTH, BATCH, SEQ, PLE_DIM), F32)
    offset = jax.random.randint(ks[2], (BATCH, 1), 0, 8192)
    positions = (offset + jnp.arange(SEQ)[None, :]).astype(jnp.int32)
    G, N = N_SSM_GROUPS, SSM_STATE
    return {
        "x": x,
        "p": p,
        "positions": positions,
        "norm1_g": 1.0 + 0.02 * nrm(ks[3], (DEPTH, D_MODEL), F32),
        "w_in": nrm(ks[4], (DEPTH, D_MODEL, IN_WIDTH), F32) * D_MODEL ** -0.5,
        "q_norm_g": 1.0 + 0.02 * nrm(ks[5], (DEPTH, HEAD_DIM), F32),
        "k_norm_g": 1.0 + 0.02 * nrm(ks[6], (DEPTH, HEAD_DIM), F32),
        "attn_sinks": 0.5 * nrm(ks[7], (DEPTH, N_Q_HEADS), F32),
        "ssm_lambda_re": -0.5 + 0.01 * nrm(ks[8], (DEPTH, G, N), F32),
        "ssm_lambda_im": math.pi * jnp.arange(N, dtype=F32)[None, None, :]
                         + 0.01 * nrm(ks[9], (DEPTH, G, N), F32),
        "ssm_b_re": nrm(ks[10], (DEPTH, G, N, SSM_GROUP), F32) * (2.0 * SSM_GROUP) ** -0.5,
        "ssm_b_im": nrm(ks[11], (DEPTH, G, N, SSM_GROUP), F32) * (2.0 * SSM_GROUP) ** -0.5,
        "ssm_c_re": nrm(ks[12], (DEPTH, G, SSM_GROUP, N), F32) * (2.0 * N) ** -0.5,
        "ssm_c_im": nrm(ks[13], (DEPTH, G, SSM_GROUP, N), F32) * (2.0 * N) ** -0.5,
        "ssm_d": nrm(ks[14], (DEPTH, G, SSM_GROUP), F32),
        "ssm_log_dt": jax.random.uniform(ks[15], (DEPTH, G), F32,
                                         minval=math.log(DT_MIN), maxval=math.log(DT_MAX)),
        "glu_w": nrm(ks[16], (DEPTH, SSM_WIDTH, SSM_WIDTH), F32) * SSM_WIDTH ** -0.5,
        "glu_b": 0.01 * nrm(ks[17], (DEPTH, SSM_WIDTH), F32),
        "attn_out_norm_g": 1.0 + 0.02 * nrm(ks[18], (DEPTH, ATTN_WIDTH), F32),
        "ssm_out_norm_g": 1.0 + 0.02 * nrm(ks[19], (DEPTH, SSM_WIDTH), F32),
        "w_out": nrm(ks[20], (DEPTH, MIX_WIDTH, D_MODEL), F32) * MIX_WIDTH ** -0.5,
        "norm2_g": 1.0 + 0.02 * nrm(ks[21], (DEPTH, D_MODEL), F32),
        "w_ffn_gate": nrm(ks[22], (DEPTH, D_MODEL, D_FF), F32) * D_MODEL ** -0.5,
        "w_ffn_up": nrm(ks[23], (DEPTH, D_MODEL, D_FF), F32) * D_MODEL ** -0.5,
        "w_ffn_down": nrm(ks[24], (DEPTH, D_FF, D_MODEL), F32) * D_FF ** -0.5,
        "ple_norm_g": 1.0 + 0.02 * nrm(ks[25], (DEPTH, D_MODEL), F32),
        "w_ple_gate": nrm(ks[26], (DEPTH, D_MODEL, D_MODEL), F32) * D_MODEL ** -0.5,
        "b_ple_gate": 0.01 * nrm(ks[27], (DEPTH, D_MODEL), F32),
        "w_ple_proj": nrm(ks[28], (DEPTH, PLE_DIM, D_MODEL), F32) * PLE_DIM ** -0.5,
    }


def reference(x, p, positions, norm1_g, w_in, q_norm_g, k_norm_g, attn_sinks,
              ssm_lambda_re, ssm_lambda_im, ssm_b_re, ssm_b_im, ssm_c_re, ssm_c_im,
              ssm_d, ssm_log_dt, glu_w, glu_b, attn_out_norm_g, ssm_out_norm_g, w_out,
              norm2_g, w_ffn_gate, w_ffn_up, w_ffn_down, ple_norm_g, w_ple_gate,
              b_ple_gate, w_ple_proj):
    B, L = x.shape[0], x.shape[1]
    h = x
    for i in range(DEPTH):
        hn = rms_norm(h, norm1_g[i])
        z = hn @ w_in[i]
        q = z[..., :ATTN_WIDTH].reshape(B, L, N_Q_HEADS, HEAD_DIM)
        k = z[..., ATTN_WIDTH:ATTN_WIDTH + KV_WIDTH].reshape(B, L, N_KV_HEADS, HEAD_DIM)
        v = z[..., ATTN_WIDTH + KV_WIDTH:ATTN_WIDTH + 2 * KV_WIDTH].reshape(B, L, N_KV_HEADS, HEAD_DIM)
        u = z[..., ATTN_WIDTH + 2 * KV_WIDTH:]
        q = partial_rotary(rms_norm(q, q_norm_g[i]), positions)
        k = partial_rotary(rms_norm(k, k_norm_g[i]), positions)
        a_out = sliding_window_attention(q, k, v, attn_sinks[i])
        s_out = s5_ssm(u, ssm_lambda_re[i], ssm_lambda_im[i], ssm_b_re[i], ssm_b_im[i],
                       ssm_c_re[i], ssm_c_im[i], ssm_d[i], ssm_log_dt[i], glu_w[i], glu_b[i])
        mix = jnp.concatenate([rms_norm(a_out, attn_out_norm_g[i]),
                               rms_norm(s_out, ssm_out_norm_g[i])], axis=-1)
        h = h + mix @ w_out[i]
        hn2 = rms_norm(h, norm2_g[i])
        h = h + (jax.nn.silu(hn2 @ w_ffn_gate[i]) * (hn2 @ w_ffn_up[i])) @ w_ffn_down[i]
        gate = jax.nn.sigmoid(rms_norm(h, ple_norm_g[i]) @ w_ple_gate[i] + b_ple_gate[i])
        h = h + gate * (p[i] @ w_ple_proj[i])
    return h
```

```python
import functools

import jax
import jax.numpy as jnp
from jax import lax
from jax.experimental import pallas as pl
from jax.experimental.pallas import tpu as pltpu

F32 = jnp.float32
BF16 = jnp.bfloat16

D_MODEL = 1024
CHUNK = 64
N_Q_HEADS = 8
N_KV_HEADS = 2
HEAD_DIM = 64
Q_REP = N_Q_HEADS // N_KV_HEADS
ATTN_WIDTH = N_Q_HEADS * HEAD_DIM
KV_WIDTH = N_KV_HEADS * HEAD_DIM
WINDOW_CHUNKS = 2
BAND = (WINDOW_CHUNKS + 1) * CHUNK
ROPE_THETA = 500000.0
ROT_DIM = HEAD_DIM // 4
ROT_HALF = ROT_DIM // 2
SSM_WIDTH = D_MODEL - ATTN_WIDTH
SSM_GROUP = 16
N_SSM_GROUPS = SSM_WIDTH // SSM_GROUP
SSM_STATE = 64
IN_WIDTH = ATTN_WIDTH + 2 * KV_WIDTH + SSM_WIDTH
PLE_DIM = 256
EPS = 1e-6
LAMBDA_RE_MAX = -1e-4
NEG_INF = -1e30

LANES = 128
MXU_DIM = 256
VMEM_LIMIT_BYTES = 56 * 1024 * 1024

SLAB_GROUPS = MXU_DIM // SSM_GROUP
N_SLABS = N_SSM_GROUPS // SLAB_GROUPS
SLAB_STATES = SLAB_GROUPS * SSM_STATE
SCAN_COLS = 512

ROW_TILE = 512
SSM_STEPS = 64


def _const_spec(shape):
    nd = len(shape)
    return pl.BlockSpec(shape, lambda *_: (0,) * nd, pipeline_mode=pl.Buffered(1))


def _dot(a, b):
    return jnp.dot(a, b, preferred_element_type=F32)


def _dot_nt(a, b):
    return lax.dot_general(a, b, (((1,), (1,)), ((), ())), preferred_element_type=F32)


def _rms(t, g):
    return t * lax.rsqrt(jnp.mean(t * t, axis=-1, keepdims=True) + EPS) * g


def _rope_table_kernel(invf_ref, pos_ref, cos_ref, sin_ref):
    pos = pos_ref[...].astype(F32)
    for f in range(ROT_HALF):
        ang = pos * invf_ref[f]
        cos_ref[f] = jnp.cos(ang)
        sin_ref[f] = jnp.sin(ang)


def _rope_tables(positions):
    B, L = positions.shape
    inv_freq = ROPE_THETA ** (-jnp.arange(ROT_HALF, dtype=F32) * (2.0 / ROT_DIM))
    cos, sin = pl.pallas_call(
        _rope_table_kernel,
        out_shape=[jax.ShapeDtypeStruct((ROT_HALF, B, L), F32)] * 2,
        in_specs=[pl.BlockSpec(memory_space=pltpu.SMEM),
                  pl.BlockSpec(memory_space=pltpu.VMEM)],
        out_specs=[pl.BlockSpec(memory_space=pltpu.VMEM)] * 2,
        name="rope_tables",
    )(inv_freq, positions)
    c = jnp.transpose(cos, (1, 2, 0))
    s = jnp.transpose(sin, (1, 2, 0))
    z8 = jnp.zeros_like(c)
    rest = HEAD_DIM - ROT_DIM
    one_r = jnp.ones((B, L, rest), F32)
    zero_r = jnp.zeros((B, L, rest), F32)
    reps = LANES // HEAD_DIM
    cosw = jnp.tile(jnp.concatenate([c, c, one_r], -1), (1, 1, reps))
    sinp = jnp.tile(jnp.concatenate([z8, s, zero_r], -1), (1, 1, reps))
    sinm = jnp.tile(jnp.concatenate([-s, z8, zero_r], -1), (1, 1, reps))
    return cosw, sinp, sinm


def _ssm_prep_kernel(lre_ref, lim_ref, ldt_ref, bre_ref, bim_ref,
                     abre_ref, abim_ref, bbre_ref, bbim_ref):
    lr = jnp.minimum(lre_ref[...], LAMBDA_RE_MAX)
    li = lim_ref[...]
    dt = jnp.exp(ldt_ref[...])
    mag = jnp.exp(lr * dt)
    ab_re = mag * jnp.cos(li * dt)
    ab_im = mag * jnp.sin(li * dt)
    nr = ab_re - 1.0
    ni = ab_im
    den = lr * lr + li * li
    f_re = (nr * lr + ni * li) / den
    f_im = (ni * lr - nr * li) / den
    abre_ref[...] = ab_re
    abim_ref[...] = ab_im
    br = bre_ref[...]
    bi = bim_ref[...]
    bbre_ref[...] = f_re * br - f_im * bi
    bbim_ref[...] = f_re * bi + f_im * br


def _ssm_params(lam_re, lam_im, b_re, b_im, c_re, c_im, log_dt):
    G, N, C = N_SSM_GROUPS, SSM_STATE, SSM_GROUP
    vm = pl.BlockSpec(memory_space=pltpu.VMEM)
    ab_re, ab_im, bb_re, bb_im = pl.pallas_call(
        _ssm_prep_kernel,
        out_shape=[jax.ShapeDtypeStruct((G, 1, N), F32)] * 2
                  + [jax.ShapeDtypeStruct((G, C, N), F32)] * 2,
        in_specs=[vm] * 5,
        out_specs=[vm] * 4,
        name="ssm_prep",
    )(lam_re.reshape(G, 1, N), lam_im.reshape(G, 1, N), log_dt.reshape(G, 1, 1),
      jnp.transpose(b_re, (0, 2, 1)), jnp.transpose(b_im, (0, 2, 1)))

    eye = jnp.eye(SLAB_GROUPS, dtype=F32)

    def bmat(bb):
        blk = bb.reshape(N_SLABS, SLAB_GROUPS, C, N)
        return jnp.einsum('kgcn,gh->kgchn', blk, eye).reshape(N_SLABS, MXU_DIM, SLAB_STATES)

    def cmat(cc):
        blk = cc.reshape(N_SLABS, SLAB_GROUPS, C, N)
        return jnp.einsum('kgcn,gh->kgnhc', blk, eye).reshape(N_SLABS, SLAB_STATES, MXU_DIM)

    bm = jnp.concatenate([bmat(bb_re), bmat(bb_im)], axis=-1).astype(BF16)
    cm = jnp.concatenate([cmat(c_re), -cmat(c_im)], axis=1).astype(BF16)
    a_re = jnp.broadcast_to(ab_re.reshape(N_SLABS, 1, SLAB_STATES), (N_SLABS, 8, SLAB_STATES))
    a_im = jnp.broadcast_to(ab_im.reshape(N_SLABS, 1, SLAB_STATES), (N_SLABS, 8, SLAB_STATES))
    return bm, cm, a_re, a_im


def _in_proj_kernel(x_ref, cos_ref, sinp_ref, sinm_ref, g1_ref, win_ref, pq_ref, pk_ref,
                    gq_ref, gk_ref, q_ref, k_ref, v_ref, u_ref):
    hn = _rms(x_ref[...], g1_ref[...]).astype(BF16)
    z = _dot(hn, win_ref[...])
    q = z[:, :ATTN_WIDTH]
    k = z[:, ATTN_WIDTH:ATTN_WIDTH + KV_WIDTH]
    v = z[:, ATTN_WIDTH + KV_WIDTH:ATTN_WIDTH + 2 * KV_WIDTH]
    u = z[:, ATTN_WIDTH + 2 * KV_WIDTH:]
    qms = _dot((q * q).astype(BF16), pq_ref[...])
    kms = _dot((k * k).astype(BF16), pk_ref[...])
    qn = q * lax.rsqrt(qms + EPS) * gq_ref[...]
    kn = k * lax.rsqrt(kms + EPS) * gk_ref[...]
    cosw, sinp, sinm = cos_ref[...], sinp_ref[...], sinm_ref[...]

    def rot(t):
        return (t * cosw + pltpu.roll(t, ROT_HALF, axis=1) * sinp
                + pltpu.roll(t, LANES - ROT_HALF, axis=1) * sinm)

    scale = HEAD_DIM ** -0.5
    for j in range(ATTN_WIDTH // LANES):
        q_ref[:, j * LANES:(j + 1) * LANES] = (rot(qn[:, j * LANES:(j + 1) * LANES]) * scale).astype(BF16)
    kr = rot(kn)
    k_ref[:, :LANES] = kr.astype(BF16)
    k_ref[:, LANES:] = pltpu.roll(kr, HEAD_DIM, axis=1).astype(BF16)
    v_ref[:, :LANES] = v.astype(BF16)
    v_ref[:, LANES:] = pltpu.roll(v, HEAD_DIM, axis=1).astype(BF16)
    u_ref[...] = u.astype(BF16)


def _in_proj(x, cosw, sinp, sinm, norm1_g, w_in, q_norm_g, k_norm_g):
    B, L, D = x.shape
    ta = ROW_TILE
    eye_q = jnp.eye(N_Q_HEADS, dtype=F32)
    eye_k = jnp.eye(N_KV_HEADS, dtype=F32)
    blk = jnp.full((HEAD_DIM, HEAD_DIM), 1.0 / HEAD_DIM, F32)
    pq = jnp.kron(eye_q, blk).astype(BF16)
    pk = jnp.kron(eye_k, blk).astype(BF16)
    gq = jnp.tile(q_norm_g, N_Q_HEADS).reshape(1, ATTN_WIDTH)
    gk = jnp.tile(k_norm_g, N_KV_HEADS).reshape(1, KV_WIDTH)
    row = lambda w: pl.BlockSpec((None, ta, w), lambda b, i: (b, i, 0))
    return pl.pallas_call(
        _in_proj_kernel,
        grid=(B, L // ta),
        out_shape=[jax.ShapeDtypeStruct((B, L, ATTN_WIDTH), BF16),
                   jax.ShapeDtypeStruct((B, L, 2 * KV_WIDTH), BF16),
                   jax.ShapeDtypeStruct((B, L, 2 * KV_WIDTH), BF16),
                   jax.ShapeDtypeStruct((L, B * SSM_WIDTH), BF16)],
        in_specs=[row(D), row(LANES), row(LANES), row(LANES),
                  _const_spec((1, D)), _const_spec((D, IN_WIDTH)),
                  _const_spec((ATTN_WIDTH, ATTN_WIDTH)), _const_spec((KV_WIDTH, KV_WIDTH)),
                  _const_spec((1, ATTN_WIDTH)), _const_spec((1, KV_WIDTH))],
        out_specs=[row(ATTN_WIDTH), row(2 * KV_WIDTH), row(2 * KV_WIDTH),
                   pl.BlockSpec((ta, SSM_WIDTH), lambda b, i: (i, b))],
        compiler_params=pltpu.CompilerParams(
            dimension_semantics=("parallel", "parallel"), vmem_limit_bytes=VMEM_LIMIT_BYTES),
        name="in_proj",
    )(x, cosw, sinp, sinm, norm1_g.reshape(1, D), w_in.astype(BF16), pq, pk, gq, gk)


def _attn_kernel(sinks_ref, q_ref, kc_ref, kp_ref, vc_ref, vp_ref, o_ref, kb_ref, vb_ref):
    n_invalid = jnp.where(pl.program_id(1) == 0, CHUNK, 0)
    halo = WINDOW_CHUNKS * CHUNK
    lane = lax.broadcasted_iota(jnp.int32, (1, LANES), 1)
    low = lane < HEAD_DIM
    zero = jnp.zeros((), BF16)
    for src_c, src_p, dst in ((kc_ref, kp_ref, kb_ref), (vc_ref, vp_ref, vb_ref)):
        for h in range(N_KV_HEADS):
            for side in range(2):
                half = h if side == 0 else 1 - h
                keep = low if side == 0 else jnp.logical_not(low)
                cols = slice(half * LANES, (half + 1) * LANES)
                dst[2 * h + side, :halo] = jnp.where(keep, src_p[:, cols], zero)
                dst[2 * h + side, halo:] = jnp.where(keep, src_c[:, cols], zero)

    rows = q_ref.shape[0]
    top = lax.broadcasted_iota(jnp.int32, (2 * CHUNK, 1), 0) < CHUNK
    key_idx = lax.broadcasted_iota(jnp.int32, (1, BAND), 1)
    for cc in range(rows // CHUNK):
        qc = q_ref[cc * CHUNK:(cc + 1) * CHUNK, :]
        band = slice(cc * CHUNK, cc * CHUNK + BAND)
        for h in range(N_KV_HEADS):
            base = h * Q_REP * HEAD_DIM
            ql = jnp.concatenate([qc[:, base:base + LANES], qc[:, base + LANES:base + 2 * LANES]], axis=0)
            acc = None
            for side in range(2):
                s = _dot_nt(ql, kb_ref[2 * h + side, band, :])
                if cc < WINDOW_CHUNKS:
                    s = jnp.where(key_idx < n_invalid * (WINDOW_CHUNKS - cc), NEG_INF, s)
                sink = jnp.where(top, sinks_ref[Q_REP * h + side], sinks_ref[Q_REP * h + 2 + side])
                m = jnp.maximum(jnp.max(s, axis=-1, keepdims=True), sink)
                p = jnp.exp(s - m)
                denom = jnp.sum(p, axis=-1, keepdims=True) + jnp.exp(sink - m)
                o = _dot(p.astype(BF16), vb_ref[2 * h + side, band, :]) / denom
                acc = o if acc is None else acc + o
            o_ref[cc * CHUNK:(cc + 1) * CHUNK, base:base + LANES] = acc[:CHUNK].astype(o_ref.dtype)
            o_ref[cc * CHUNK:(cc + 1) * CHUNK, base + LANES:base + 2 * LANES] = acc[CHUNK:].astype(o_ref.dtype)


def _attention(q, k2, v2, sinks):
    B, L, _ = q.shape
    tb = ROW_TILE
    halo = WINDOW_CHUNKS * CHUNK
    per = tb // halo
    cur = lambda w: pl.BlockSpec((None, tb, w), lambda b, i: (b, i, 0))
    prev = pl.BlockSpec((None, halo, 2 * KV_WIDTH), lambda b, i: (b, jnp.maximum(i * per - 1, 0), 0))
    return pl.pallas_call(
        _attn_kernel,
        grid=(B, L // tb),
        out_shape=jax.ShapeDtypeStruct((B, L, ATTN_WIDTH), BF16),
        in_specs=[pl.BlockSpec(memory_space=pltpu.SMEM), cur(ATTN_WIDTH),
                  cur(2 * KV_WIDTH), prev, cur(2 * KV_WIDTH), prev],
        out_specs=cur(ATTN_WIDTH),
        scratch_shapes=[pltpu.VMEM((2 * N_KV_HEADS, halo + tb, LANES), BF16)] * 2,
        compiler_params=pltpu.CompilerParams(
            dimension_semantics=("parallel", "parallel"), vmem_limit_bytes=VMEM_LIMIT_BYTES),
        name="swa_attn",
    )(sinks, q, k2, k2, v2, v2)


def _ssm_kernel(u_ref, bm_ref, cm_ref, are_ref, aim_ref, d_ref, gluw_ref, glub_ref, o_ref,
                bu_ref, s_ref, st_ref, y_ref, *, batch, steps):
    @pl.when(pl.program_id(0) == 0)
    def _():
        st_ref[...] = jnp.zeros_like(st_ref)

    reps = batch // 8
    for k in range(N_SLABS):
        ucols = slice(k * MXU_DIM, (k + 1) * MXU_DIM)
        uk = u_ref[:, ucols]
        bu_ref[...] = _dot(uk, bm_ref[k])
        for j in range(SLAB_STATES // SCAN_COLS):
            cre = slice(j * SCAN_COLS, (j + 1) * SCAN_COLS)
            cim = slice(SLAB_STATES + j * SCAN_COLS, SLAB_STATES + (j + 1) * SCAN_COLS)
            ar = jnp.concatenate([are_ref[k, :, cre]] * reps, axis=0)
            ai = jnp.concatenate([aim_ref[k, :, cre]] * reps, axis=0)

            def step(t, carry):
                sr, si = carry
                r0 = pl.multiple_of(t * batch, batch)
                nsr = ar * sr - ai * si + bu_ref[pl.ds(r0, batch), cre]
                nsi = ar * si + ai * sr + bu_ref[pl.ds(r0, batch), cim]
                s_ref[pl.ds(r0, batch), cre] = nsr.astype(BF16)
                s_ref[pl.ds(r0, batch), cim] = nsi.astype(BF16)
                return nsr, nsi

            sr, si = lax.fori_loop(0, steps, step, (st_ref[k, :, cre], st_ref[k, :, cim]), unroll=4)
            st_ref[k, :, cre] = sr
            st_ref[k, :, cim] = si
        yk = _dot(s_ref[...], cm_ref[k])
        y_ref[:, ucols] = yk + d_ref[:, ucols] * uk.astype(F32)
    y = jax.nn.gelu(y_ref[...])
    gate = jax.nn.sigmoid(_dot(y.astype(BF16), gluw_ref[...]) + glub_ref[...])
    o_ref[...] = (y * gate).astype(o_ref.dtype)


def _ssm(u_tm, batch, bm, cm, a_re, a_im, d_skip, glu_w, glu_b):
    rows_total = u_tm.shape[0]
    steps = SSM_STEPS
    rows = steps * batch
    tile = pl.BlockSpec((rows, SSM_WIDTH), lambda i: (i, 0))
    return pl.pallas_call(
        functools.partial(_ssm_kernel, batch=batch, steps=steps),
        grid=(rows_total // rows,),
        out_shape=jax.ShapeDtypeStruct((rows_total, SSM_WIDTH), BF16),
        in_specs=[tile, _const_spec(bm.shape), _const_spec(cm.shape), _const_spec(a_re.shape),
                  _const_spec(a_im.shape), _const_spec((1, SSM_WIDTH)),
                  _const_spec((SSM_WIDTH, SSM_WIDTH)), _const_spec((1, SSM_WIDTH))],
        out_specs=tile,
        scratch_shapes=[pltpu.VMEM((rows, 2 * SLAB_STATES), F32),
                        pltpu.VMEM((rows, 2 * SLAB_STATES), BF16),
                        pltpu.VMEM((N_SLABS, batch, 2 * SLAB_STATES), F32),
                        pltpu.VMEM((rows, SSM_WIDTH), F32)],
        compiler_params=pltpu.CompilerParams(
            dimension_semantics=("arbitrary",), vmem_limit_bytes=VMEM_LIMIT_BYTES),
        name="s5_ssm",
    )(u_tm, bm, cm, a_re, a_im, d_skip.reshape(1, SSM_WIDTH), glu_w.astype(BF16),
      glu_b.reshape(1, SSM_WIDTH))


def _out_kernel(x_ref, a_ref, s_ref, p_ref, ga_ref, gs_ref, wo_ref, g2_ref, wg_ref, wu_ref, wd_ref,
                g3_ref, wpg_ref, bpg_ref, wpp_ref, o_ref):
    an = _rms(a_ref[...].astype(F32), ga_ref[...]).astype(BF16)
    sn = _rms(s_ref[...].astype(F32), gs_ref[...]).astype(BF16)
    h = x_ref[...] + _dot(an, wo_ref[:ATTN_WIDTH, :]) + _dot(sn, wo_ref[ATTN_WIDTH:, :])
    hn2 = _rms(h, g2_ref[...]).astype(BF16)
    act = (jax.nn.silu(_dot(hn2, wg_ref[...])) * _dot(hn2, wu_ref[...])).astype(BF16)
    h = h + _dot(act, wd_ref[...])
    hn3 = _rms(h, g3_ref[...]).astype(BF16)
    gate = jax.nn.sigmoid(_dot(hn3, wpg_ref[...]) + bpg_ref[...])
    o_ref[...] = h + gate * _dot(p_ref[...].astype(BF16), wpp_ref[...])


def _out_ffn_ple(x, a_out, s_tm, p, ga, gs, w_out, g2, wg, wu, wd, g3, wpg, bpg, wpp):
    B, L, D = x.shape
    td = ROW_TILE
    d_ff = wg.shape[1]
    row = lambda w: pl.BlockSpec((None, td, w), lambda b, i: (b, i, 0))
    vec = lambda w: _const_spec((1, w))
    return pl.pallas_call(
        _out_kernel,
        grid=(B, L // td),
        out_shape=jax.ShapeDtypeStruct((B, L, D), F32),
        in_specs=[row(D), row(ATTN_WIDTH),
                  pl.BlockSpec((td, SSM_WIDTH), lambda b, i: (i, b)),
                  row(PLE_DIM), vec(ATTN_WIDTH), vec(SSM_WIDTH), _const_spec((D, D)), vec(D),
                  _const_spec((D, d_ff)), _const_spec((D, d_ff)), _const_spec((d_ff, D)), vec(D),
                  _const_spec((D, D)), vec(D), _const_spec((PLE_DIM, D))],
        out_specs=row(D),
        compiler_params=pltpu.CompilerParams(
            dimension_semantics=("parallel", "parallel"), vmem_limit_bytes=VMEM_LIMIT_BYTES),
        name="out_ffn_ple",
    )(x, a_out, s_tm, p, ga.reshape(1, -1), gs.reshape(1, -1), w_out.astype(BF16), g2.reshape(1, D),
      wg.astype(BF16), wu.astype(BF16), wd.astype(BF16), g3.reshape(1, D), wpg.astype(BF16),
      bpg.reshape(1, D), wpp.astype(BF16))


def kernel(x, p, positions, norm1_g, w_in, q_norm_g, k_norm_g, attn_sinks, ssm_lambda_re, ssm_lambda_im, ssm_b_re, ssm_b_im, ssm_c_re, ssm_c_im, ssm_d, ssm_log_dt, glu_w, glu_b, attn_out_norm_g, ssm_out_norm_g, w_out, norm2_g, w_ffn_gate, w_ffn_up, w_ffn_down, ple_norm_g, w_ple_gate, b_ple_gate, w_ple_proj):
    B, L, _ = x.shape
    depth = norm1_g.shape[0]
    cosw, sinp, sinm = _rope_tables(positions)
    h = x
    for i in range(depth):
        bm, cm, a_re, a_im = _ssm_params(ssm_lambda_re[i], ssm_lambda_im[i], ssm_b_re[i], ssm_b_im[i],
                                         ssm_c_re[i], ssm_c_im[i], ssm_log_dt[i])
        q, k2, v2, u_tm = _in_proj(h, cosw, sinp, sinm, norm1_g[i], w_in[i], q_norm_g[i], k_norm_g[i])
        a_out = _attention(q, k2, v2, attn_sinks[i])
        s_tm = _ssm(u_tm.reshape(L * B, SSM_WIDTH), B, bm, cm, a_re, a_im, ssm_d[i], glu_w[i], glu_b[i])
        h = _out_ffn_ple(h, a_out, s_tm.reshape(L, B * SSM_WIDTH), p[i], attn_out_norm_g[i],
                         ssm_out_norm_g[i], w_out[i], norm2_g[i], w_ffn_gate[i], w_ffn_up[i],
                         w_ffn_down[i], ple_norm_g[i], w_ple_gate[i], b_ple_gate[i], w_ple_proj[i])
    return h
```
